```python
import math
import jax, jax.numpy as jnp
from jax import lax
import numpy as np

D_MODEL = 4096
BATCH = 1
SEQ = 8192
DEPTH = 1
DEC_BATCH = 128
DEC_SEQ = 1
PAST_LEN = 2048
PAGE_SIZE = 128

H_A = 16
DK_A = 128
DV_A = 128
H_B = 8
D_B = 128
H_I = 16
D_I = 64
H_M = 4
D_M = 256
N_MEM = 256
TOPK_MAX = 256
Q_BLOCK = 128
HGRN_CHUNK = 64
D_FF = 11008
CONV_W = 3
EPS = 1e-6
INDEX_SCALE = (H_I ** -0.5) * (D_I ** -0.5)

kernel_name = 'hymba_hgrn2_dsa_memory_convffn_step'


def _in_sizes():
    return (H_A * DK_A, H_A * DK_A, H_A * DV_A, H_A * DV_A,
            H_B * D_B, H_B * D_B, H_B * D_B,
            H_I * D_I, D_I, H_I, H_M * D_M)


def _split_cols(z):
    offs, acc = [], 0
    for s in _in_sizes()[:-1]:
        acc += s
        offs.append(acc)
    return jnp.split(z, offs, axis=-1)


def rmsnorm(x, g):
    xf = x.astype(jnp.float32)
    y = xf * lax.rsqrt(jnp.mean(xf * xf, axis=-1, keepdims=True) + EPS)
    return (y * g.astype(jnp.float32)).astype(x.dtype)


def mixer_inputs(x, g_mix, w_in, lb, g_qn_b, g_kn_b, g_qn_m):
    B, T, _ = x.shape
    h = rmsnorm(x, g_mix)
    z = jnp.einsum('btd,de->bte', h, w_in)
    f_pre, q_a, i_a, gate_a, q_b, k_b, v_b, q_i, k_i, w_i, q_m = _split_cols(z)
    f = lb + (1.0 - lb) * jax.nn.sigmoid(f_pre.astype(jnp.float32))
    q_a = jax.nn.silu(q_a).reshape(B, T, H_A, DK_A)
    k_a = (1.0 - f).reshape(B, T, H_A, DK_A)
    logf_a = jnp.log(f).reshape(B, T, H_A, DK_A)
    v_a = i_a.reshape(B, T, H_A, DV_A)
    q_b = rmsnorm(q_b.reshape(B, T, H_B, D_B), g_qn_b)
    k_b = rmsnorm(k_b.reshape(B, T, H_B, D_B), g_kn_b)
    v_b = v_b.reshape(B, T, H_B, D_B)
    q_i = q_i.reshape(B, T, H_I, D_I)
    w_i = w_i * INDEX_SCALE
    q_m = rmsnorm(q_m.reshape(B, T, H_M, D_M), g_qn_m)
    return q_a, k_a, v_a, logf_a, gate_a, q_b, k_b, v_b, q_i, k_i, w_i, q_m


def hgrn2_recurrence(q, k, v, log_f, s0):
    B, T, H, DK = q.shape
    DV = v.shape[-1]
    C = math.gcd(T, HGRN_CHUNK)
    n = T // C

    def to_chunks(a):
        return a.astype(jnp.float32).reshape(B, n, C, H, a.shape[-1]).transpose(1, 0, 3, 2, 4)

    causal = jnp.tril(jnp.ones((C, C), dtype=bool))

    def step(S, inp):
        qc, kc, vc, gc = inp
        b = jnp.cumsum(gc, axis=2)
        diff = b[:, :, :, None, :] - b[:, :, None, :, :]
        decay = jnp.exp(jnp.where(causal[:, :, None], diff, -jnp.inf))
        a = jnp.einsum('bhtd,bhsd,bhtsd->bhts', qc, kc, decay)
        o = (jnp.einsum('bhts,bhsv->bhtv', a, vc)
             + jnp.einsum('bhtd,bhdv->bhtv', qc * jnp.exp(b), S))
        b_last = b[:, :, -1, :]
        S = (jnp.exp(b_last)[..., None] * S
             + jnp.einsum('bhsd,bhsv->bhdv', kc * jnp.exp(b_last[:, :, None, :] - b), vc))
        return S, o

    S, o = lax.scan(step, s0.astype(jnp.float32),
                    (to_chunks(q), to_chunks(k), to_chunks(v), to_chunks(log_f)))
    o = o.transpose(1, 0, 3, 2, 4).reshape(B, T, H, DV)
    return o, S


def hgrn2_output(o, gate, g_out_a):
    B, T = o.shape[:2]
    g = jax.nn.silu(gate.astype(jnp.float32)).reshape(B, T, H_A, DV_A)
    return (rmsnorm(o, g_out_a) * g).reshape(B, T, H_A * DV_A)


def indexer_select(q_i, w_i, k_i, q_pos, topk):
    logits = jnp.einsum('bqhd,bsd->bqhs', q_i.astype(jnp.float32), k_i.astype(jnp.float32))
    score = jnp.einsum('bqh,bqhs->bqs', w_i.astype(jnp.float32), jax.nn.relu(logits))
    visible = jnp.arange(k_i.shape[1])[None, None, :] <= q_pos[None, :, None]
    score = jnp.where(visible, score, -jnp.inf)
    _, idx = lax.top_k(score, topk)
    valid = idx <= q_pos[None, :, None]
    return idx, valid


def attend_selected(q, k_sel, v_sel, valid):
    s = jnp.einsum('bqhd,bqkhd->bqhk', q.astype(jnp.float32), k_sel.astype(jnp.float32)) * (D_B ** -0.5)
    s = jnp.where(valid[:, :, None, :], s, -jnp.inf)
    p = jax.nn.softmax(s, axis=-1)
    return jnp.einsum('bqhk,bqkhd->bqhd', p, v_sel.astype(jnp.float32))


def prompt_sparse_attention(q_b, k_b, v_b, q_i, k_i, w_i):
    B, S = q_b.shape[:2]
    n_blk = S // Q_BLOCK
    topk = min(TOPK_MAX, S // 4)
    b_idx = jnp.arange(B)[:, None, None]

    def blocks(a):
        return a.reshape((B, n_blk, Q_BLOCK) + a.shape[2:]).swapaxes(0, 1)

    def one_block(args):
        start, qb, qi, wi = args
        idx, valid = indexer_select(qi, wi, k_i, start + jnp.arange(Q_BLOCK), topk)
        return attend_selected(qb, k_b[b_idx, idx], v_b[b_idx, idx], valid)

    out = lax.map(one_block, (jnp.arange(n_blk) * Q_BLOCK, blocks(q_b), blocks(q_i), blocks(w_i)))
    return out.swapaxes(0, 1).reshape(B, S, H_B * D_B)


def sample_sparse_attention(q_b, k_b, v_b, q_i, k_i, w_i, cache_k, cache_v, cache_kidx, page_table):
    DB, DS = q_b.shape[:2]
    past = page_table.shape[1] * PAGE_SIZE
    topk = min(TOPK_MAX, (past + DS) // 4)
    b_idx = jnp.arange(DB)[:, None, None]
    ki_past = cache_kidx[page_table].reshape(DB, past, D_I)
    ki_all = jnp.concatenate([ki_past.astype(k_i.dtype), k_i], axis=1)
    idx, valid = indexer_select(q_i, w_i, ki_all, past + jnp.arange(DS), topk)
    in_past = (idx < past)[..., None, None]
    p_idx = jnp.minimum(idx, past - 1)
    phys = page_table[b_idx, p_idx // PAGE_SIZE]
    off = p_idx % PAGE_SIZE
    n_idx = jnp.clip(idx - past, 0, DS - 1)
    k_sel = jnp.where(in_past, cache_k[phys, off].astype(k_b.dtype), k_b[b_idx, n_idx])
    v_sel = jnp.where(in_past, cache_v[phys, off].astype(v_b.dtype), v_b[b_idx, n_idx])
    return attend_selected(q_b, k_sel, v_sel, valid).reshape(DB, DS, H_B * D_B)


def memory_kv(mem, g_mem, w_mem_kv, g_kn_m):
    B, N, _ = mem.shape
    kv = jnp.einsum('bnd,de->bne', rmsnorm(mem, g_mem), w_mem_kv)
    k, v = jnp.split(kv, 2, axis=-1)
    return rmsnorm(k.reshape(B, N, H_M, D_M), g_kn_m), v.reshape(B, N, H_M, D_M)


def memory_attention(q_m, mem_k, mem_v):
    B, T = q_m.shape[:2]
    s = jnp.einsum('bthd,bnhd->bthn', q_m.astype(jnp.float32), mem_k.astype(jnp.float32)) * (D_M ** -0.5)
    p = jax.nn.softmax(s, axis=-1)
    o = jnp.einsum('bthn,bnhd->bthd', p, mem_v.astype(jnp.float32))
    return o.reshape(B, T, H_M * D_M)


def merge_groups(x, o_a, o_b, o_m, w_out):
    o = jnp.concatenate([o_a.astype(x.dtype), o_b.astype(x.dtype), o_m.astype(x.dtype)], axis=-1)
    return x + jnp.einsum('bte,ed->btd', o, w_out)


def conv_ffn(x, conv_prev, g_ffn, w_gu, conv_w, conv_b, w_down):
    T = x.shape[1]
    h = rmsnorm(x, g_ffn)
    gate, up = jnp.split(jnp.einsum('btd,df->btf', h, w_gu), [D_FF], axis=-1)
    ext = jnp.concatenate([conv_prev.astype(gate.dtype), gate], axis=1)
    conv = conv_b + conv_w[0] * ext[:, 0:T]
    for j in range(1, CONV_W):
        conv = conv + conv_w[j] * ext[:, j:j + T]
    y = jnp.einsum('btf,fd->btd', jax.nn.silu(conv) * up, w_down)
    return x + y.astype(x.dtype), ext[:, T:]


def setup_inputs(seed: int = 0) -> dict:
    key = jax.random.key(seed)
    ks = jax.random.split(key, 32)
    f32 = jnp.float32
    n_pages = PAST_LEN // PAGE_SIZE
    n_used = DEC_BATCH * n_pages
    n_pool = n_used + max(1, n_used // 4)
    d_in = sum(_in_sizes())

    def nrm(k, shape, scale=1.0):
        return jax.random.normal(k, shape, f32) * scale

    def gain(k, shape):
        return 1.0 + 0.02 * jax.random.normal(k, shape, f32)

    page_table = jax.random.permutation(ks[0], n_pool)[:n_used].reshape(DEC_BATCH, n_pages).astype(jnp.int32)
    return {
        'x_prompt': nrm(ks[1], (BATCH, SEQ, D_MODEL)),
        'x_sample': nrm(ks[2], (DEC_BATCH, DEC_SEQ, D_MODEL)),
        'mem_prompt': nrm(ks[3], (BATCH, N_MEM, D_MODEL)),
        'cache_k': nrm(ks[4], (DEPTH, n_pool, PAGE_SIZE, H_B, D_B)),
        'cache_v': nrm(ks[5], (DEPTH, n_pool, PAGE_SIZE, H_B, D_B)),
        'cache_kidx': nrm(ks[6], (DEPTH, n_pool, PAGE_SIZE, D_I)),
        'cache_mem_k': nrm(ks[7], (DEPTH, DEC_BATCH, N_MEM, H_M, D_M)),
        'cache_mem_v': nrm(ks[8], (DEPTH, DEC_BATCH, N_MEM, H_M, D_M)),
        'state_hgrn': nrm(ks[9], (DEPTH, DEC_BATCH, H_A, DK_A, DV_A), 0.3),
        'state_conv': nrm(ks[10], (DEPTH, DEC_BATCH, CONV_W - 1, D_FF)),
        'page_table': page_table,
        'g_mix': gain(ks[11], (DEPTH, D_MODEL)),
        'w_in': nrm(ks[12], (DEPTH, D_MODEL, d_in), D_MODEL ** -0.5),
        'lb_param': nrm(ks[13], (DEPTH + 1, H_A * DK_A), 0.5),
        'g_out_a': gain(ks[14], (DEPTH, DV_A)),
        'g_qn_b': gain(ks[15], (DEPTH, D_B)),
        'g_kn_b': gain(ks[16], (DEPTH, D_B)),
        'g_mem': gain(ks[17], (DEPTH, D_MODEL)),
        'w_mem_kv': nrm(ks[18], (DEPTH, D_MODEL, 2 * H_M * D_M), D_MODEL ** -0.5),
        'g_qn_m': gain(ks[19], (DEPTH, D_M)),
        'g_kn_m': gain(ks[20], (DEPTH, D_M)),
        'w_out': nrm(ks[21], (DEPTH, H_A * DV_A + H_B * D_B + H_M * D_M, D_MODEL), D_MODEL ** -0.5),
        'g_ffn': gain(ks[22], (DEPTH, D_MODEL)),
        'w_gu': nrm(ks[23], (DEPTH, D_MODEL, 2 * D_FF), D_MODEL ** -0.5),
        'conv_w': nrm(ks[24], (DEPTH, CONV_W, D_FF), CONV_W ** -0.5),
        'conv_b': nrm(ks[25], (DEPTH, D_FF), 0.01),
        'w_down': nrm(ks[26], (DEPTH, D_FF, D_MODEL), D_FF ** -0.5),
    }


def reference(x_prompt, x_sample, mem_prompt, cache_k, cache_v, cache_kidx, cache_mem_k, cache_mem_v,
              state_hgrn, state_conv, page_table, g_mix, w_in, lb_param, g_out_a, g_qn_b, g_kn_b,
              g_mem, w_mem_kv, g_qn_m, g_kn_m, w_out, g_ffn, w_gu, conv_w, conv_b, w_down):
    lb_all = jnp.cumsum(jax.nn.softmax(lb_param.astype(jnp.float32), axis=0), axis=0)
    xp, xs = x_prompt, x_sample
    B = xp.shape[0]
    kp, vp, kip, mkp, mvp, hp, cp = [], [], [], [], [], [], []
    ksl, vsl, kisl, hsl, csl = [], [], [], [], []
    for l in range(DEPTH):
        lb = lb_all[l]
        q_a, k_a, v_a, lf_a, gt_a, q_b, k_b, v_b, q_i, k_i, w_i, q_m = mixer_inputs(
            xp, g_mix[l], w_in[l], lb, g_qn_b[l], g_kn_b[l], g_qn_m[l])
        o_a, s_a = hgrn2_recurrence(q_a, k_a, v_a, lf_a, jnp.zeros((B, H_A, DK_A, DV_A), jnp.float32))
        o_a = hgrn2_output(o_a, gt_a, g_out_a[l])
        o_b = prompt_sparse_attention(q_b, k_b, v_b, q_i, k_i, w_i)
        mk, mv = memory_kv(mem_prompt, g_mem[l], w_mem_kv[l], g_kn_m[l])
        o_m = memory_attention(q_m, mk, mv)
        xp = merge_groups(xp, o_a, o_b, o_m, w_out[l])
        xp, c_new = conv_ffn(xp, jnp.zeros((B, CONV_W - 1, D_FF), xp.dtype),
                             g_ffn[l], w_gu[l], conv_w[l], conv_b[l], w_down[l])
        kp.append(k_b); vp.append(v_b); kip.append(k_i); mkp.append(mk); mvp.append(mv)
        hp.append(s_a); cp.append(c_new)
        q_a, k_a, v_a, lf_a, gt_a, q_b, k_b, v_b, q_i, k_i, w_i, q_m = mixer_inputs(
            xs, g_mix[l], w_in[l], lb, g_qn_b[l], g_kn_b[l], g_qn_m[l])
        o_a, s_a = hgrn2_recurrence(q_a, k_a, v_a, lf_a, state_hgrn[l])
        o_a = hgrn2_output(o_a, gt_a, g_out_a[l])
        o_b = sample_sparse_attention(q_b, k_b, v_b, q_i, k_i, w_i,
                                      cache_k[l], cache_v[l], cache_kidx[l], page_table)
        o_m = memory_attention(q_m, cache_mem_k[l], cache_mem_v[l])
        xs = merge_groups(xs, o_a, o_b, o_m, w_out[l])
        xs, c_new = conv_ffn(xs, state_conv[l], g_ffn[l], w_gu[l], conv_w[l], conv_b[l], w_down[l])
        ksl.append(k_b); vsl.append(v_b); kisl.append(k_i); hsl.append(s_a); csl.append(c_new)
    y_prompt, y_sample = xp, xs
    k_prompt, v_prompt, kidx_prompt = jnp.stack(kp), jnp.stack(vp), jnp.stack(kip)
    mem_k_prompt, mem_v_prompt = jnp.stack(mkp), jnp.stack(mvp)
    hgrn_prompt, conv_prompt = jnp.stack(hp), jnp.stack(cp)
    k_sample, v_sample, kidx_sample = jnp.stack(ksl), jnp.stack(vsl), jnp.stack(kisl)
    hgrn_sample, conv_sample = jnp.stack(hsl), jnp.stack(csl)
    return (y_prompt, y_sample, k_prompt, v_prompt, kidx_prompt, mem_k_prompt, mem_v_prompt,
            hgrn_prompt, conv_prompt, k_sample, v_sample, kidx_sample, hgrn_sample, conv_sample)
```

```python
import functools
import math

import numpy as np
import jax
import jax.numpy as jnp
from jax import lax
from jax.experimental import pallas as pl
from jax.experimental.pallas import tpu as pltpu

F32, BF16, I32 = jnp.float32, jnp.bfloat16, jnp.int32
EPS = 1e-6
TOPK_MAX = 256
INT_MIN = -(2 ** 31)
MASKED = -2e30
M_INIT = -1e30
LANES = 128
SUBLANES = 8
VMEM_LIMIT = 56 * 2 ** 20

_NT = (((1,), (1,)), ((), ()))
_TN = (((0,), (0,)), ((), ()))


def _cp(*sem):
    return pltpu.CompilerParams(dimension_semantics=sem, vmem_limit_bytes=VMEM_LIMIT)


def _tile(n, pref):
    if n <= pref:
        return n
    t = pref
    while n % t:
        t //= 2
    return t


def _dot(a, b):
    return jnp.dot(a, b, preferred_element_type=F32)


def _dg(a, b, dims):
    return lax.dot_general(a, b, dims, preferred_element_type=F32)


def _silu(x):
    return x * jax.nn.sigmoid(x)


def _split2(x):
    hi = x.astype(BF16)
    lo = (x - hi.astype(F32)).astype(BF16)
    return hi, lo


def _full(a):
    return pl.BlockSpec(a.shape, lambda *_: (0,) * a.ndim)


def _cast_body(x_ref, o_ref):
    o_ref[...] = x_ref[...].astype(o_ref.dtype)


def cast_bf16(w, ncols=None):
    K, N = w.shape
    ncols = N if ncols is None else ncols
    tk, tn = _tile(K, 512), _tile(ncols, 1024)
    return pl.pallas_call(
        _cast_body, grid=(K // tk, ncols // tn),
        in_specs=[pl.BlockSpec((tk, tn), lambda i, j: (i, j))],
        out_specs=pl.BlockSpec((tk, tn), lambda i, j: (i, j)),
        out_shape=jax.ShapeDtypeStruct((K, ncols), BF16),
        compiler_params=_cp("parallel", "parallel"), name="cast_bf16")(w)


def _rms_body(x_ref, g_ref, o_ref):
    x = x_ref[...]
    ms = jnp.mean(x * x, axis=-1, keepdims=True)
    o_ref[...] = (x * lax.rsqrt(ms + EPS) * g_ref[...]).astype(o_ref.dtype)


def rms_cast(x, g):
    M, D = x.shape
    tm = _tile(M, 256)
    return pl.pallas_call(
        _rms_body, grid=(M // tm,),
        in_specs=[pl.BlockSpec((tm, D), lambda i: (i, 0)), pl.BlockSpec((1, D), lambda i: (0, 0))],
        out_specs=pl.BlockSpec((tm, D), lambda i: (i, 0)),
        out_shape=jax.ShapeDtypeStruct((M, D), BF16),
        compiler_params=_cp("parallel"), name="rms_cast")(x, g.reshape(1, D))


def _mm_body(*refs, has_res):
    if has_res:
        x_ref, w_ref, r_ref, o_ref = refs
    else:
        x_ref, w_ref, o_ref = refs
    acc = _dot(x_ref[...], w_ref[...])
    if has_res:
        acc = r_ref[...] + acc
    o_ref[...] = acc


def matmul(x, w, res=None, tm=1024, tn=512):
    M, K = x.shape
    N = w.shape[1]
    tm, tn = _tile(M, tm), _tile(N, tn)
    in_specs = [pl.BlockSpec((tm, K), lambda n, m: (m, 0)), pl.BlockSpec((K, tn), lambda n, m: (0, n))]
    args = [x, w]
    if res is not None:
        in_specs.append(pl.BlockSpec((tm, tn), lambda n, m: (m, n)))
        args.append(res)
    return pl.pallas_call(
        functools.partial(_mm_body, has_res=res is not None), grid=(N // tn, M // tm),
        in_specs=in_specs, out_specs=pl.BlockSpec((tm, tn), lambda n, m: (m, n)),
        out_shape=jax.ShapeDtypeStruct((M, N), F32),
        compiler_params=_cp("parallel", "parallel"), name="matmul")(*args)


def _headnorm(x, g, nh, hd, scale=1.0):
    outs = []
    for h in range(nh):
        xh = x[:, h * hd:(h + 1) * hd]
        ms = jnp.mean(xh * xh, axis=-1, keepdims=True)
        y = xh * lax.rsqrt(ms + EPS) * g
        outs.append(y * scale if scale != 1.0 else y)
    return jnp.concatenate(outs, axis=-1)


def _prep_body(qb_ref, kb_ref, vb_ref, qi_ref, qm_ref, kiw_ref, gq_ref, gk_ref, gm_ref,
               qbn_ref, qbb_ref, kn_ref, knb_ref, vbb_ref, qih_ref, qil_ref, qmn_ref, qmb_ref, kih_ref, kil_ref,
               *, nhb, hdb, nhm, hdm):
    qbn = _headnorm(qb_ref[...], gq_ref[...], nhb, hdb, hdb ** -0.5)
    qbn_ref[...] = qbn
    qbb_ref[...] = qbn.astype(BF16)
    kn = _headnorm(kb_ref[...], gk_ref[...], nhb, hdb)
    kn_ref[...] = kn
    knb_ref[...] = kn.astype(BF16)
    vbb_ref[...] = vb_ref[...].astype(BF16)
    qih_ref[...], qil_ref[...] = _split2(qi_ref[...])
    qmn = _headnorm(qm_ref[...], gm_ref[...], nhm, hdm, hdm ** -0.5)
    qmn_ref[...] = qmn
    qmb_ref[...] = qmn.astype(BF16)
    kih_ref[...], kil_ref[...] = _split2(kiw_ref[...])


def prep_attention(z, zt, g_qn_b, g_kn_b, g_qn_m, dm):
    M = z.shape[0]
    wb, wm = dm.wb, dm.wm
    c0 = dm.off_qb // wb
    tm = _tile(M, 256)
    row = lambda c: pl.BlockSpec((tm, wb), lambda i, c=c: (i, c))
    gq, gk, gm = g_qn_b.reshape(1, dm.D_B), g_kn_b.reshape(1, dm.D_B), g_qn_m.reshape(1, dm.D_M)
    out_shapes = [
        jax.ShapeDtypeStruct((M, wb), F32),
        jax.ShapeDtypeStruct((M, wb), BF16),
        jax.ShapeDtypeStruct((M, wb), F32),
        jax.ShapeDtypeStruct((M, wb), BF16),
        jax.ShapeDtypeStruct((M, wb), BF16),
        jax.ShapeDtypeStruct((M, wb), BF16),
        jax.ShapeDtypeStruct((M, wb), BF16),
        jax.ShapeDtypeStruct((M, wm), F32),
        jax.ShapeDtypeStruct((M, wm), BF16),
        jax.ShapeDtypeStruct((M, LANES), BF16),
        jax.ShapeDtypeStruct((M, LANES), BF16),
    ]
    out_specs = [pl.BlockSpec((tm, s.shape[1]), lambda i: (i, 0)) for s in out_shapes]
    return pl.pallas_call(
        functools.partial(_prep_body, nhb=dm.H_B, hdb=dm.D_B, nhm=dm.H_M, hdm=dm.D_M), grid=(M // tm,),
        in_specs=[row(c0), row(c0 + 1), row(c0 + 2), row(c0 + 3),
                  pl.BlockSpec((tm, wm), lambda i: (i, 0)),
                  pl.BlockSpec((tm, LANES), lambda i: (i, wm // LANES)),
                  _full(gq), _full(gk), _full(gm)],
        out_specs=out_specs, out_shape=out_shapes,
        compiler_params=_cp("parallel"), name="prep_attention")(z, z, z, z, zt, zt, gq, gk, gm)


def _memk_body(k_ref, v_ref, g_ref, kn_ref, knb_ref, vb_ref, *, nh, hd):
    kn = _headnorm(k_ref[...], g_ref[...], nh, hd)
    kn_ref[...] = kn
    knb_ref[...] = kn.astype(BF16)
    vb_ref[...] = v_ref[...].astype(BF16)


def prep_memory(kv, g_kn_m, nh, hd):
    N = kv.shape[0]
    w = nh * hd
    g = g_kn_m.reshape(1, hd)
    blk = lambda c: pl.BlockSpec((N, w), lambda i, c=c: (0, c))
    return pl.pallas_call(
        functools.partial(_memk_body, nh=nh, hd=hd), grid=(1,),
        in_specs=[blk(0), blk(1), _full(g)],
        out_specs=[blk(0), blk(0), blk(0)],
        out_shape=[jax.ShapeDtypeStruct((N, w), F32), jax.ShapeDtypeStruct((N, w), BF16),
                   jax.ShapeDtypeStruct((N, w), BF16)],
        compiler_params=_cp("arbitrary"), name="prep_memory")(kv, kv, g)


def _hgrn_consts(C):
    nl = int(math.log2(C))
    assert 1 << nl == C
    t = np.arange(C)
    W = np.zeros((nl + 2, C, C), np.float32)
    for l in range(nl):
        m = 1 << l
        r = (t // (2 * m)) * (2 * m) + m - 1
        for i in range(C):
            if (i // m) % 2 == 1:
                W[l, i, r[i] + 1:i + 1] = 1
            else:
                W[l, i, i + 1:r[i] + 1] = 1
    W[nl] = np.tril(np.ones((C, C)))
    W[nl + 1] = np.triu(np.ones((C, C)), 1)
    x = t[:, None] ^ t[None, :]
    lev = np.where(t[:, None] > t[None, :], np.floor(np.log2(np.maximum(x, 1))), -1).astype(np.int32)
    return jnp.asarray(W.reshape((nl + 2) * C, C), BF16), jnp.asarray(lev)


def _hgrn_gates(fp, qa, lb):
    f = lb + (1.0 - lb) * jax.nn.sigmoid(fp)
    return f, _silu(qa)


def _hgrn_out(o, gate, g_out):
    ms = jnp.mean(o * o, axis=-1, keepdims=True)
    return (o * lax.rsqrt(ms + EPS) * g_out) * _silu(gate)


def _hgrn_body(fp_ref, qa_ref, ia_ref, ga_ref, lb_ref, go_ref, w_ref, lev_ref, o_ref, s_ref, st_ref, *, C, nl):
    c = pl.program_id(1)

    @pl.when(c == 0)
    def _():
        st_ref[...] = jnp.zeros_like(st_ref)

    f, q = _hgrn_gates(fp_ref[...], qa_ref[...], lb_ref[...])
    k = 1.0 - f
    v = ia_ref[...]
    g = jnp.log(f)
    g1 = g.astype(BF16)
    r1 = g - g1.astype(F32)
    g2 = r1.astype(BF16)
    g3 = (r1 - g2.astype(F32)).astype(BF16)
    W = w_ref[...]
    G = _dot(W, g1) + _dot(W, g2) + _dot(W, g3)
    lev = lev_ref[...]
    A = jnp.zeros((C, C), F32)
    for l in range(nl):
        E = jnp.exp(G[l * C:(l + 1) * C])
        P = _dg((q * E).astype(BF16), (k * E).astype(BF16), _NT)
        A = jnp.where(lev == l, P, A)
    b = G[nl * C:(nl + 1) * C]
    bs = G[(nl + 1) * C:(nl + 2) * C]
    vb = v.astype(BF16)
    st = st_ref[...]
    o = (_dot(A.astype(BF16), vb) + jnp.sum(q * k, axis=-1, keepdims=True) * v
         + _dg((q * jnp.exp(b)).astype(BF16), st.astype(BF16), _NT))
    st_new = st * jnp.exp(b[C - 1:C, :]) + _dg(vb, (k * jnp.exp(bs)).astype(BF16), _TN)
    st_ref[...] = st_new
    s_ref[0] = st_new
    o_ref[...] = _hgrn_out(o, ga_ref[...], go_ref[...]).astype(o_ref.dtype)


def hgrn_prompt(z, lb, g_out_a, dm, C=128):
    T = z.shape[0]
    H, D = dm.H_A, dm.DK_A
    C = _tile(T, C)
    nl = int(math.log2(C))
    W, lev = _hgrn_consts(C)
    col = lambda k: pl.BlockSpec((C, D), lambda h, c, k=k: (c, k * H + h))
    return pl.pallas_call(
        functools.partial(_hgrn_body, C=C, nl=nl), grid=(H, T // C),
        in_specs=[col(0), col(1), col(2), col(3),
                  pl.BlockSpec((1, D), lambda h, c: (0, h)), _full(g_out_a.reshape(1, D)), _full(W), _full(lev)],
        out_specs=[pl.BlockSpec((C, D), lambda h, c: (c, h)), pl.BlockSpec((1, D, D), lambda h, c: (h, 0, 0))],
        out_shape=[jax.ShapeDtypeStruct((T, H * D), BF16), jax.ShapeDtypeStruct((H, D, D), F32)],
        scratch_shapes=[pltpu.VMEM((D, D), F32)],
        compiler_params=_cp("parallel", "arbitrary"), name="hgrn_prompt")(
            z, z, z, z, lb.reshape(1, H * D), g_out_a.reshape(1, D), W, lev)


def _hgrn_step_body(fp_ref, qa_ref, lb_ref, ia_ref, ga_ref, go_ref, s_ref, o_ref, so_ref, orow_ref, *, bb):
    f, q = _hgrn_gates(fp_ref[0, 0], qa_ref[0, 0], lb_ref[0])
    k = 1.0 - f
    v = ia_ref[...]
    for j in range(bb):
        sn = f[:, j:j + 1] * s_ref[j, 0] + k[:, j:j + 1] * v[j:j + 1, :]
        so_ref[j, 0] = sn
        orow_ref[j:j + 1, :] = jnp.sum(q[:, j:j + 1] * sn, axis=0, keepdims=True)
    o_ref[...] = _hgrn_out(orow_ref[...], ga_ref[...], go_ref[...]).astype(o_ref.dtype)


def hgrn_step(z, state, lb, g_out_a, dm, bb=16):
    B = z.shape[0]
    H, D = dm.H_A, dm.DK_A
    bb = _tile(B, bb)
    tr = lambda a: a.reshape(B // bb, bb, H, D).transpose(0, 2, 3, 1)
    fpT, qaT = tr(z[:, :H * D]), tr(z[:, H * D:2 * H * D])
    tsp = pl.BlockSpec((1, 1, D, bb), lambda g, h: (g, h, 0, 0))
    ssp = pl.BlockSpec((bb, 1, D, D), lambda g, h: (g, h, 0, 0))
    return pl.pallas_call(
        functools.partial(_hgrn_step_body, bb=bb), grid=(B // bb, H),
        in_specs=[tsp, tsp, pl.BlockSpec((1, D, 1), lambda g, h: (h, 0, 0)),
                  pl.BlockSpec((bb, D), lambda g, h: (g, 2 * H + h)), pl.BlockSpec((bb, D), lambda g, h: (g, 3 * H + h)),
                  _full(g_out_a.reshape(1, D)), ssp],
        out_specs=[pl.BlockSpec((bb, D), lambda g, h: (g, h)), ssp],
        out_shape=[jax.ShapeDtypeStruct((B, H * D), BF16), jax.ShapeDtypeStruct(state.shape, F32)],
        scratch_shapes=[pltpu.VMEM((bb, D), F32)],
        compiler_params=_cp("parallel", "parallel"), name="hgrn_step")(
            fpT, qaT, lb.reshape(H, D, 1), z, z, g_out_a.reshape(1, D), state)


def _sort_key(x):
    bits = lax.bitcast_convert_type(x, I32)
    return bits ^ ((bits >> 31) & 0x7FFFFFFF)


def _count(sc_ref, nt, tk, pred):
    tq = sc_ref.shape[0]

    def body(j, cnt):
        k0 = pl.multiple_of(j * tk, tk)
        hit = jnp.where(pred(sc_ref[:, pl.ds(k0, tk)], k0), 1.0, 0.0)
        c = hit[:, 0:LANES]
        for u in range(1, tk // LANES):
            c = c + hit[:, u * LANES:(u + 1) * LANES]
        return cnt + c

    cnt = lax.fori_loop(0, nt, body, jnp.zeros((tq, LANES), F32))
    return jnp.sum(cnt, axis=1, keepdims=True)


def _select_topk(sc_ref, nt, tk, topk, idx_bits):
    tq = sc_ref.shape[0]
    kf = float(topk)
    cols = lambda k0: k0 + lax.broadcasted_iota(I32, (tq, tk), 1)
    n_ge = lambda cand: _count(sc_ref, nt, tk, lambda key, k0: key >= cand)
    zero = jnp.zeros((tq, 1), I32)
    prefix = jnp.where(n_ge(zero) >= kf, zero, jnp.full((tq, 1), INT_MIN, I32))

    def bit_step(i, prefix):
        cand = prefix | lax.shift_left(jnp.int32(1), 30 - i)
        return jnp.where(n_ge(cand) >= kf, cand, prefix)

    prefix = lax.fori_loop(0, 31, bit_step, prefix)
    thr = jnp.maximum(prefix, INT_MIN + 1)

    @pl.when(jnp.max(n_ge(thr)) > kf)
    def _():
        need = kf - n_ge(thr + 1)
        n_lt = lambda p: _count(sc_ref, nt, tk, lambda key, k0: (key == thr) & (cols(k0) < p))

        def idx_step(i, p):
            cand = p | lax.shift_left(jnp.int32(1), idx_bits - 1 - i)
            return jnp.where(n_lt(cand) < need, cand, p)

        last = lax.fori_loop(0, idx_bits, idx_step, zero)

        def demote(j, carry):
            k0 = pl.multiple_of(j * tk, tk)
            key = sc_ref[:, pl.ds(k0, tk)]
            sc_ref[:, pl.ds(k0, tk)] = jnp.where((key == thr) & (cols(k0) > last), key - 1, key)
            return carry

        lax.fori_loop(0, nt, demote, 0)

    return thr


def _pattn_body(q3_ref, kiw_ref, qb_ref, kit_ref, k_ref, v_ref, o_ref, sc_ref, m_ref, l_ref, acc_ref,
                *, tq, tk, topk, nh, hd, nhi, di, wscale, idx_bits):
    q0 = pl.program_id(0) * tq
    nt = (q0 + tq + tk - 1) // tk
    row_ids = q0 + lax.broadcasted_iota(I32, (tq, tk), 0)
    lane_ids = lax.broadcasted_iota(I32, (tq, tk), 1)
    w = kiw_ref[:, di:di + nhi] * wscale

    def score_tile(j, carry):
        k0 = pl.multiple_of(j * tk, tk)
        kt = kit_ref[:, pl.ds(k0, tk)]
        acc = jnp.zeros((tq, tk), F32)
        for h in range(nhi):
            acc = acc + w[:, h:h + 1] * jnp.maximum(_dot(q3_ref[h], kt), 0.0)
        sc_ref[:, pl.ds(k0, tk)] = jnp.where(k0 + lane_ids <= row_ids, _sort_key(acc), INT_MIN)
        return carry

    lax.fori_loop(0, nt, score_tile, 0)
    thr = _select_topk(sc_ref, nt, tk, topk, idx_bits)

    m_ref[...] = jnp.full(m_ref.shape, M_INIT, F32)
    l_ref[...] = jnp.zeros_like(l_ref)
    acc_ref[...] = jnp.zeros_like(acc_ref)

    def attend_tile(j, carry):
        k0 = pl.multiple_of(j * tk, tk)
        sel = sc_ref[:, pl.ds(k0, tk)] >= thr
        for h in range(nh):
            hs = slice(h * hd, (h + 1) * hd)
            s = jnp.where(sel, _dg(qb_ref[:, hs], k_ref[pl.ds(k0, tk), hs], _NT), MASKED)
            m_old = m_ref[h]
            m_new = jnp.maximum(m_old, jnp.max(s, axis=-1, keepdims=True))
            alpha = jnp.exp(m_old - m_new)
            p = jnp.exp(s - m_new)
            l_ref[h] = alpha * l_ref[h] + jnp.sum(p, axis=-1, keepdims=True)
            acc_ref[h] = alpha * acc_ref[h] + _dot(p.astype(BF16), v_ref[pl.ds(k0, tk), hs])
            m_ref[h] = m_new
        return carry

    lax.fori_loop(0, nt, attend_tile, 0)
    for h in range(nh):
        o_ref[:, h * hd:(h + 1) * hd] = (acc_ref[h] / l_ref[h]).astype(o_ref.dtype)


def prompt_sparse_attention(q3, kiw, qb, kit, kb, vb, dm, tq=128, tk=256):
    S = qb.shape[0]
    tq = _tile(S, tq)
    tk = _tile(S, tk)
    topk = min(TOPK_MAX, S // 4)
    nh, hd, nhi, di = dm.H_B, dm.D_B, dm.H_I, dm.D_I
    once = dict(pipeline_mode=pl.Buffered(1))
    body = functools.partial(_pattn_body, tq=tq, tk=tk, topk=topk, nh=nh, hd=hd, nhi=nhi, di=di,
                             wscale=dm.index_scale, idx_bits=max(1, (S - 1).bit_length()))
    return pl.pallas_call(
        body, grid=(S // tq,),
        in_specs=[pl.BlockSpec((nhi, tq, 3 * di), lambda i: (0, i, 0)),
                  pl.BlockSpec((tq, LANES), lambda i: (i, 0)),
                  pl.BlockSpec((tq, nh * hd), lambda i: (i, 0)),
                  pl.BlockSpec(kit.shape, lambda i: (0, 0), **once),
                  pl.BlockSpec(kb.shape, lambda i: (0, 0), **once),
                  pl.BlockSpec(vb.shape, lambda i: (0, 0), **once)],
        out_specs=pl.BlockSpec((tq, nh * hd), lambda i: (i, 0)),
        out_shape=jax.ShapeDtypeStruct((S, nh * hd), BF16),
        scratch_shapes=[pltpu.VMEM((tq, S), I32), pltpu.VMEM((nh, tq, 1), F32), pltpu.VMEM((nh, tq, 1), F32),
                        pltpu.VMEM((nh, tq, hd), F32)],
        compiler_params=_cp("arbitrary"), name="prompt_sparse_attention")(q3, kiw, qb, kit, kb, vb)


def _mattn_body(q_ref, k_ref, v_ref, o_ref, *, nh, hd):
    for h in range(nh):
        hs = slice(h * hd, (h + 1) * hd)
        s = _dg(q_ref[:, hs], k_ref[:, hs], _NT)
        p = jnp.exp(s - jnp.max(s, axis=-1, keepdims=True))
        o = _dot(p.astype(BF16), v_ref[:, hs]) / jnp.sum(p, axis=-1, keepdims=True)
        o_ref[:, hs] = o.astype(o_ref.dtype)


def memory_attention(qm, mk, mv, nh, hd):
    M = qm.shape[0]
    tm = _tile(M, 512)
    return pl.pallas_call(
        functools.partial(_mattn_body, nh=nh, hd=hd), grid=(M // tm,),
        in_specs=[pl.BlockSpec((tm, nh * hd), lambda i: (i, 0)), _full(mk), _full(mv)],
        out_specs=pl.BlockSpec((tm, nh * hd), lambda i: (i, 0)),
        out_shape=jax.ShapeDtypeStruct((M, nh * hd), BF16),
        compiler_params=_cp("parallel"), name="memory_attention")(qm, mk, mv)


def _sidx_body(pt_ref, qh_ref, ql_ref, qf_ref, w_ref, kn_ref, *rest, npg, wscale):
    page_refs, sc_ref = rest[:npg], rest[npg]
    qh, ql = qh_ref[0], ql_ref[0]
    w = w_ref[0] * wscale
    for p in range(npg):
        kh, kl = _split2(page_refs[p][0])
        lg = _dg(qh, kh, _NT) + _dg(ql, kh, _NT) + _dg(qh, kl, _NT)
        sc_ref[0, p:p + 1, :] = jnp.sum(w * jnp.maximum(lg, 0.0), axis=0, keepdims=True)
    ls = jnp.sum(qf_ref[0] * kn_ref[0], axis=-1, keepdims=True)
    s_self = jnp.sum(w * jnp.maximum(ls, 0.0), axis=0, keepdims=True)
    lane = lax.broadcasted_iota(I32, (1, LANES), 1)
    sc_ref[0, npg:npg + 1, :] = jnp.where(lane == 0, s_self, 0.0)


def sample_index_scores(qh, ql, qf, w, knew, cache_kidx, page_table, dm):
    B, npg = page_table.shape
    pg, di = cache_kidx.shape[1:]
    assert pg == LANES
    nhi = dm.H_I
    per_b = lambda shape: pl.BlockSpec((1,) + shape, lambda b, pt: (b, 0, 0))
    page = lambda p: pl.BlockSpec((1, pg, di), lambda b, pt, p=p: (pt[b, p], 0, 0))
    grid_spec = pltpu.PrefetchScalarGridSpec(
        num_scalar_prefetch=1, grid=(B,),
        in_specs=[per_b((nhi, di)), per_b((nhi, di)), per_b((nhi, di)), per_b((nhi, 1)), per_b((1, di))]
                 + [page(p) for p in range(npg)],
        out_specs=per_b((npg + 1, pg)))
    return pl.pallas_call(
        functools.partial(_sidx_body, npg=npg, wscale=dm.index_scale), grid_spec=grid_spec,
        out_shape=jax.ShapeDtypeStruct((B, npg + 1, pg), F32),
        compiler_params=_cp("arbitrary"), name="sample_index_scores")(
            page_table, qh, ql, qf, w, knew, *([cache_kidx] * npg))


def _sthr_body(sc_ref, key_ref, thr_ref, *, ncand, tk, topk, idx_bits):
    tq, width = sc_ref.shape
    for j in range(width // tk):
        cs = slice(j * tk, (j + 1) * tk)
        cols = j * tk + lax.broadcasted_iota(I32, (tq, tk), 1)
        key_ref[:, cs] = jnp.where(cols < ncand, _sort_key(sc_ref[:, cs]), INT_MIN)
    thr_ref[...] = _select_topk(key_ref, width // tk, tk, topk, idx_bits)


def sample_select(sc, ncand, topk):
    B, width = sc.shape
    body = functools.partial(_sthr_body, ncand=ncand, tk=LANES, topk=topk, idx_bits=max(1, (width - 1).bit_length()))
    return pl.pallas_call(
        body, grid=(1,), in_specs=[_full(sc)],
        out_specs=[pl.BlockSpec((B, width), lambda i: (0, 0)), pl.BlockSpec((B, 1), lambda i: (0, 0))],
        out_shape=[jax.ShapeDtypeStruct((B, width), I32), jax.ShapeDtypeStruct((B, 1), I32)],
        compiler_params=_cp("arbitrary"), name="sample_select")(sc)


def _dattn_body(pt_ref, thr_ref, q_ref, kn_ref, vn_ref, key_ref, ones_ref, rep_ref, *rest, pp, npg):
    k_refs, v_refs = rest[:pp], rest[pp:2 * pp]
    o_ref, m_ref, l_ref, acc_ref = rest[2 * pp:]
    b, j = pl.program_id(0), pl.program_id(1)
    thr = thr_ref[b]
    q = q_ref[0]

    @pl.when(j == 0)
    def _():
        s_self = jnp.sum(q * kn_ref[0], axis=-1, keepdims=True)
        sel = key_ref[0, npg:npg + 1, 0:1] >= thr
        m_ref[...] = jnp.broadcast_to(jnp.where(sel, s_self, M_INIT), m_ref.shape)
        l_ref[...] = jnp.broadcast_to(jnp.where(sel, 1.0, 0.0), l_ref.shape)
        acc_ref[...] = jnp.where(sel, vn_ref[0], 0.0)

    ones, rep = ones_ref[...], rep_ref[...]
    eye = lax.broadcasted_iota(I32, (LANES, LANES), 0) == lax.broadcasted_iota(I32, (LANES, LANES), 1)
    for u in range(pp):
        kp = k_refs[u][0]
        pg, nh, hd = kp.shape
        hi, lo = _split2((kp * q[None]).reshape(pg * nh, hd))
        keyrow = key_ref[0, pl.ds(j * pp + u, 1), :]
        bias = jnp.where(keyrow >= thr, 0.0, MASKED)
        dmat = _dot(jnp.where(eye, jnp.broadcast_to(bias, (LANES, LANES)), 0.0).astype(BF16), ones)
        s = (_dot(hi, ones) + _dot(lo, ones) + _dot(rep, dmat.astype(BF16))).reshape(pg, nh, hd)
        m_old = m_ref[...]
        m_new = jnp.maximum(m_old, jnp.max(s, axis=0))
        alpha = jnp.exp(m_old - m_new)
        p = jnp.exp(s - m_new[None])
        l_ref[...] = alpha * l_ref[...] + jnp.sum(p, axis=0)
        acc_ref[...] = alpha * acc_ref[...] + jnp.sum(p * v_refs[u][0], axis=0)
        m_ref[...] = m_new

    @pl.when(j == pl.num_programs(1) - 1)
    def _():
        o_ref[0] = acc_ref[...] / l_ref[...]


def sample_sparse_attention(q, knew, vnew, keys, thr, cache_k, cache_v, page_table, pp=4):
    B, npg = page_table.shape
    _, pg, nh, hd = cache_k.shape
    assert pg == LANES and hd == LANES and nh == SUBLANES
    pp = _tile(npg, pp)
    ones = jnp.ones((hd, LANES), BF16)
    rep = jnp.asarray(np.repeat(np.eye(pg, dtype=np.float32), nh, axis=0), BF16)
    per_b = lambda shape: pl.BlockSpec((1,) + shape, lambda b, j, pt, th: (b, 0, 0))
    page = lambda u: pl.BlockSpec((1, pg, nh, hd), lambda b, j, pt, th, u=u: (pt[b, j * pp + u], 0, 0, 0))
    const = lambda a: pl.BlockSpec(a.shape, lambda b, j, pt, th: (0, 0))
    grid_spec = pltpu.PrefetchScalarGridSpec(
        num_scalar_prefetch=2, grid=(B, npg // pp),
        in_specs=[per_b((nh, hd)), per_b((nh, hd)), per_b((nh, hd)), per_b((npg + 1, pg)), const(ones), const(rep)]
                 + [page(u) for u in range(pp)] * 2,
        out_specs=per_b((nh, hd)),
        scratch_shapes=[pltpu.VMEM((nh, hd), F32)] * 3)
    return pl.pallas_call(
        functools.partial(_dattn_body, pp=pp, npg=npg), grid_spec=grid_spec,
        out_shape=jax.ShapeDtypeStruct((B, nh, hd), F32),
        compiler_params=_cp("arbitrary", "arbitrary"), name="sample_sparse_attention")(
            page_table, thr, q, knew, vnew, keys, ones, rep, *([cache_k] * pp), *([cache_v] * pp))


def _smem_body(q_ref, k_ref, v_ref, ho_ref, o_ref):
    hi, lo = _split2(k_ref[0] * q_ref[0])
    ho = ho_ref[...]
    s = _dot(hi, ho) + _dot(lo, ho)
    p = jnp.exp(s - jnp.max(s, axis=0, keepdims=True))
    o_ref[0] = jnp.sum(p * v_ref[0], axis=0, keepdims=True) / jnp.sum(p, axis=0, keepdims=True)


def sample_memory_attention(q, mem_k, mem_v, nh, hd):
    B, N, w = mem_k.shape
    ho = jnp.asarray(np.kron(np.eye(nh, dtype=np.float32), np.ones((hd, hd), np.float32)), BF16)
    return pl.pallas_call(
        _smem_body, grid=(B,),
        in_specs=[pl.BlockSpec((1, 1, w), lambda b: (b, 0, 0)), pl.BlockSpec((1, N, w), lambda b: (b, 0, 0)),
                  pl.BlockSpec((1, N, w), lambda b: (b, 0, 0)), _full(ho)],
        out_specs=pl.BlockSpec((1, 1, w), lambda b: (b, 0, 0)),
        out_shape=jax.ShapeDtypeStruct((B, 1, w), F32),
        compiler_params=_cp("parallel"), name="sample_memory_attention")(q, mem_k, mem_v, ho)


def _ffn_body(x_ref, wg_ref, wu_ref, cw_ref, cb_ref, *rest, sample):
    x = x_ref[...]
    gate = _dot(x, wg_ref[...])
    up = _dot(x, wu_ref[...])
    cw, cb = cw_ref[...], cb_ref[...]
    if sample:
        s0_ref, s1_ref, act_ref, gate_ref = rest
        conv = cb + cw[0:1] * s0_ref[...] + cw[1:2] * s1_ref[...] + cw[2:3] * gate
        gate_ref[...] = gate
    else:
        act_ref, tail_ref, carry_ref = rest
        tm = gate.shape[0]

        @pl.when(pl.program_id(1) == 0)
        def _():
            carry_ref[...] = jnp.zeros_like(carry_ref)

        prev2, prev1 = carry_ref[SUBLANES - 2:SUBLANES - 1, :], carry_ref[SUBLANES - 1:SUBLANES, :]
        rows = lax.broadcasted_iota(I32, gate.shape, 0)
        g1 = jnp.where(rows == 0, prev1, pltpu.roll(gate, 1, 0))
        g2 = jnp.where(rows == 0, prev2, jnp.where(rows == 1, prev1, pltpu.roll(gate, 2, 0)))
        conv = cb + cw[0:1] * g2 + cw[1:2] * g1 + cw[2:3] * gate
        carry_ref[...] = gate[tm - SUBLANES:tm, :]
        tail_ref[...] = gate[tm - SUBLANES:tm, :]
    act_ref[...] = (_silu(conv) * up).astype(act_ref.dtype)


def conv_ffn_act(h, w_gu, conv_w, conv_b, state=None, tm=1024, tn=256):
    M, D = h.shape
    F = w_gu.shape[1] // 2
    tm, tn = _tile(M, tm), _tile(F, tn)
    nf = F // tn
    col = lambda c0: pl.BlockSpec((D, tn), lambda n, m, c0=c0: (0, n + c0))
    ctile = lambda r: pl.BlockSpec((r, tn), lambda n, m: (0, n))
    mtile = pl.BlockSpec((tm, tn), lambda n, m: (m, n))
    in_specs = [pl.BlockSpec((tm, D), lambda n, m: (m, 0)), col(0), col(nf), ctile(conv_w.shape[0]), ctile(1)]
    args = [h, w_gu, w_gu, conv_w, conv_b.reshape(1, F)]
    if state is None:
        out_specs = [mtile, ctile(SUBLANES)]
        out_shape = [jax.ShapeDtypeStruct((M, F), BF16), jax.ShapeDtypeStruct((SUBLANES, F), F32)]
        scratch = [pltpu.VMEM((SUBLANES, tn), F32)]
    else:
        in_specs += [mtile, mtile]
        args += list(state)
        out_specs = [mtile, mtile]
        out_shape = [jax.ShapeDtypeStruct((M, F), BF16), jax.ShapeDtypeStruct((M, F), F32)]
        scratch = []
    return pl.pallas_call(
        functools.partial(_ffn_body, sample=state is not None), grid=(nf, M // tm),
        in_specs=in_specs, out_specs=out_specs, out_shape=out_shape, scratch_shapes=scratch,
        compiler_params=_cp("arbitrary", "arbitrary"), name="conv_ffn_act")(*args)


class _Dims:
    def __init__(self, state_hgrn, cache_k, cache_kidx, cache_mem_k, w_in):
        self.H_A, self.DK_A, self.DV_A = state_hgrn.shape[2:]
        self.H_B, self.D_B = cache_k.shape[3:]
        self.D_I = cache_kidx.shape[-1]
        self.H_M, self.D_M = cache_mem_k.shape[3:]
        self.wa, self.wb, self.wm = self.H_A * self.DK_A, self.H_B * self.D_B, self.H_M * self.D_M
        d_in = w_in.shape[-1]
        self.H_I = (d_in - 4 * self.wa - 3 * self.wb - self.D_I - self.wm) // (self.D_I + 1)
        self.off_qb = 4 * self.wa
        self.off_ki = self.off_qb + 3 * self.wb + self.H_I * self.D_I
        self.off_qm = self.off_ki + self.D_I + self.H_I
        self.index_scale = (self.H_I ** -0.5) * (self.D_I ** -0.5)
        assert self.DK_A == self.DV_A == LANES and self.D_B == LANES
        assert self.H_I * self.D_I == self.wb and self.off_qb % self.wb == 0
        assert self.D_I + self.H_I <= LANES and self.off_qm + self.wm == d_in


def _trunk(x, dm, w_main, w_tail, g_mix):
    h = rms_cast(x, g_mix)
    return matmul(h, w_main), matmul(h, w_tail)


def _index_operands(qih, qil, kih, kil, dm):
    M = qih.shape[0]
    r = lambda a: a.reshape(M, dm.H_I, dm.D_I)
    q3 = jnp.concatenate([r(qih), r(qil), r(qih)], axis=-1).transpose(1, 0, 2)
    k3 = jnp.concatenate([kih[:, :dm.D_I], kih[:, :dm.D_I], kil[:, :dm.D_I]], axis=-1)
    return q3, k3


def kernel(x_prompt, x_sample, mem_prompt, cache_k, cache_v, cache_kidx, cache_mem_k, cache_mem_v, state_hgrn, state_conv, page_table, g_mix, w_in, lb_param, g_out_a, g_qn_b, g_kn_b, g_mem, w_mem_kv, g_qn_m, g_kn_m, w_out, g_ffn, w_gu, conv_w, conv_b, w_down):
    depth = w_in.shape[0]
    assert depth == 1 and x_prompt.shape[0] == 1 and x_sample.shape[1] == 1
    dm = _Dims(state_hgrn, cache_k, cache_kidx, cache_mem_k, w_in)
    S, D = x_prompt.shape[1:]
    B = x_sample.shape[0]
    n_mem = mem_prompt.shape[1]
    past = page_table.shape[1] * cache_k.shape[2]
    l = 0
    lb = jnp.cumsum(jax.nn.softmax(lb_param.astype(F32), axis=0), axis=0)[l]

    w_main = cast_bf16(w_in[l], dm.off_ki)
    pad = LANES - dm.D_I - dm.H_I
    w_tail = cast_bf16(jnp.concatenate(
        [w_in[l][:, dm.off_qm:], w_in[l][:, dm.off_ki:dm.off_qm], jnp.zeros((D, pad), F32)], axis=1))
    w_mem_b, w_out_b, w_gu_b, w_down_b = (cast_bf16(w[l]) for w in (w_mem_kv, w_out, w_gu, w_down))

    xp, xs = x_prompt[0], x_sample[:, 0]

    z, zt = _trunk(xp, dm, w_main, w_tail, g_mix[l])
    o_a, st_p = hgrn_prompt(z, lb, g_out_a[l], dm)
    (_, qbb, kn_p, knb, vbb, qih, qil, _, qmb, kih, kil) = prep_attention(z, zt, g_qn_b[l], g_kn_b[l], g_qn_m[l], dm)
    q3, k3 = _index_operands(qih, qil, kih, kil, dm)
    o_b = prompt_sparse_attention(q3, zt[:, dm.wm:], qbb, k3.T, knb, vbb, dm)
    kv_mem = matmul(rms_cast(mem_prompt[0], g_mem[l]), w_mem_b)
    mk, mkb, mvb = prep_memory(kv_mem, g_kn_m[l], dm.H_M, dm.D_M)
    o_m = memory_attention(qmb, mkb, mvb, dm.H_M, dm.D_M)
    x1 = matmul(jnp.concatenate([o_a, o_b, o_m], axis=1), w_out_b, res=xp)
    act, tail = conv_ffn_act(rms_cast(x1, g_ffn[l]), w_gu_b, conv_w[l], conv_b[l])
    y_p = matmul(act, w_down_b, res=x1, tm=256)

    zs, zts = _trunk(xs, dm, w_main, w_tail, g_mix[l])
    o_as, st_s = hgrn_step(zs, state_hgrn[l], lb, g_out_a[l], dm)
    (qbn_s, _, kn_s, _, _, qih_s, qil_s, qmn_s, _, _, _) = prep_attention(zs, zts, g_qn_b[l], g_kn_b[l], g_qn_m[l], dm)
    hsplit = lambda a: a.reshape(B, dm.H_I, dm.D_I)
    ki_s = zts[:, dm.wm:dm.wm + dm.D_I]
    sc = sample_index_scores(
        hsplit(qih_s), hsplit(qil_s), hsplit(zs[:, dm.off_qb + 3 * dm.wb:dm.off_ki]),
        zts[:, dm.wm + dm.D_I:dm.wm + dm.D_I + dm.H_I].reshape(B, dm.H_I, 1), ki_s.reshape(B, 1, dm.D_I),
        cache_kidx[l], page_table, dm)
    keys, thr = sample_select(sc.reshape(B, -1), past + 1, min(TOPK_MAX, (past + 1) // 4))
    hb = lambda a: a.reshape(B, dm.H_B, dm.D_B)
    vb_s = zs[:, dm.off_qb + 2 * dm.wb:dm.off_qb + 3 * dm.wb]
    o_bs = sample_sparse_attention(hb(qbn_s), hb(kn_s), hb(vb_s), keys.reshape(sc.shape), thr.reshape(B),
                                   cache_k[l], cache_v[l], page_table)
    o_ms = sample_memory_attention(qmn_s.reshape(B, 1, dm.wm), cache_mem_k[l].reshape(B, n_mem, dm.wm),
                                   cache_mem_v[l].reshape(B, n_mem, dm.wm), dm.H_M, dm.D_M)
    o_s = jnp.concatenate([o_as, o_bs.reshape(B, dm.wb).astype(BF16), o_ms.reshape(B, dm.wm).astype(BF16)], axis=1)
    x1s = matmul(o_s, w_out_b, res=xs)
    s0, s1 = state_conv[l][:, 0], state_conv[l][:, 1]
    act_s, gate_s = conv_ffn_act(rms_cast(x1s, g_ffn[l]), w_gu_b, conv_w[l], conv_b[l], state=(s0, s1))
    y_s = matmul(act_s, w_down_b, res=x1s, tm=256)

    hb_p = lambda a: a.reshape(1, 1, S, dm.H_B, dm.D_B)
    hb_s = lambda a: a.reshape(1, B, 1, dm.H_B, dm.D_B)
    vb_p = z[:, dm.off_qb + 2 * dm.wb:dm.off_qb + 3 * dm.wb]
    return (
        y_p[None], y_s[:, None],
        hb_p(kn_p), hb_p(vb_p), zt[:, dm.wm:dm.wm + dm.D_I].reshape(1, 1, S, dm.D_I),
        mk.reshape(1, 1, n_mem, dm.H_M, dm.D_M), kv_mem[:, dm.wm:].reshape(1, 1, n_mem, dm.H_M, dm.D_M),
        jnp.swapaxes(st_p, -1, -2)[None, None], tail[SUBLANES - 2:][None, None],
        hb_s(kn_s), hb_s(vb_s), ki_s.reshape(1, B, 1, dm.D_I),
        st_s[None], jnp.stack([s1, gate_s], axis=1)[None],
    )
```
